```python
import jax, jax.numpy as jnp
from jax import lax
import numpy as np

D_MODEL = 1024
BATCH = 8
SEQ = 2048
DEPTH = 2

CHUNK = 64
RMS_EPS = 1e-6

LRU_HEADS = 8
LRU_WIDTH = D_MODEL // 2
LRU_HEAD_DIM = LRU_WIDTH // LRU_HEADS
LRU_CONV = 4
LRU_C = 8.0
SC_HEADS = 8
SC_WIDTH = D_MODEL // 2
SC_CONV = 3
EVEN_IN = 2 * LRU_WIDTH + 3 * SC_WIDTH

POOL_WINDOWS = (2, 4, 8, 16)
POOL_WIDTH = D_MODEL // 2
POOL_GROUP = POOL_WIDTH // len(POOL_WINDOWS)
SG_HEADS = 8
SG_WIDTH = D_MODEL // 2
SG_HEAD_DIM = SG_WIDTH // SG_HEADS
SG_BLOCK = 128
ODD_IN = POOL_WIDTH + 2 * SG_WIDTH

FFN_HIDDEN = -(-8 * D_MODEL // (3 * 256)) * 256

N_EVEN = (DEPTH + 1) // 2
N_ODD = DEPTH // 2

kernel_name = "hybrid_rglru_shortconv_pool_gmlp_trunk"


def rms_norm(x, g):
    xf = x.astype(jnp.float32)
    y = xf * lax.rsqrt(jnp.mean(xf * xf, axis=-1, keepdims=True) + RMS_EPS)
    return y.astype(x.dtype) * g


def causal_dwconv(x, w):
    k = w.shape[0]
    return lax.conv_general_dilated(
        x, w[:, None, :], window_strides=(1,), padding=((k - 1, 0),),
        dimension_numbers=("NWC", "WIO", "NWC"), feature_group_count=x.shape[-1])


def rg_lru(x, w_r, b_r, w_i, b_i, lam):
    bsz, s, _ = x.shape
    xh = x.reshape(bsz, s, LRU_HEADS, LRU_HEAD_DIM)
    r = jax.nn.sigmoid(jnp.einsum("bshd,hde->bshe", xh, w_r) + b_r).reshape(bsz, s, LRU_WIDTH)
    i = jax.nn.sigmoid(jnp.einsum("bshd,hde->bshe", xh, w_i) + b_i).reshape(bsz, s, LRU_WIDTH)
    log_a = -LRU_C * r.astype(jnp.float32) * jax.nn.softplus(-lam.astype(jnp.float32))
    a = jnp.exp(log_a)
    b = jnp.sqrt(-jnp.expm1(2.0 * log_a)) * (i * x).astype(jnp.float32)

    def combine(left, right):
        a1, b1 = left
        a2, b2 = right
        return a1 * a2, a2 * b1 + b2

    _, h = lax.associative_scan(combine, (a, b), axis=1)
    return h.astype(x.dtype)


def multiscale_pool(x, w, scale):
    bsz, s, _ = x.shape
    xf = x.astype(jnp.float32).reshape(bsz, s, len(POOL_WINDOWS), POOL_GROUP)
    cs = jnp.cumsum(xf, axis=1)
    count = jnp.arange(1, s + 1, dtype=jnp.float32)
    outs = []
    for g, win in enumerate(POOL_WINDOWS):
        c = cs[:, :, g]
        lag = jnp.pad(c, ((0, 0), (win, 0), (0, 0)))[:, :s]
        mean = (c - lag) / jnp.minimum(count, float(win))[None, :, None]
        outs.append(mean - xf[:, :, g])
    p = jnp.stack(outs, axis=2).astype(x.dtype)
    y = jnp.einsum("bsgc,gcd->bsgd", p, w).reshape(bsz, s, POOL_WIDTH)
    return y * scale


def spatial_gate(v, w_s, b_s):
    bsz, s, _ = v.shape
    vb = v.reshape(bsz, s // SG_BLOCK, SG_BLOCK, SG_HEADS, SG_HEAD_DIM)
    ck = np.arange(SG_BLOCK) // CHUNK
    mask = ck[None, :] <= ck[:, None]
    ws = jnp.where(mask[None], w_s, 0)
    g = jnp.einsum("hts,bnshc->bnthc", ws, vb) + b_s.T[:, :, None]
    return g.reshape(bsz, s, SG_WIDTH)


def even_mixer(h, w_in, conv_w, conv_b, w_r, b_r, w_i, b_i, lam, sc_conv_w, w_out):
    z = h @ w_in
    xa, ga, bg, cg, hs = jnp.split(
        z, [LRU_WIDTH, 2 * LRU_WIDTH, 2 * LRU_WIDTH + SC_WIDTH, 2 * LRU_WIDTH + 2 * SC_WIDTH], axis=-1)
    xa = causal_dwconv(xa, conv_w) + conv_b
    ya = rg_lru(xa, w_r, b_r, w_i, b_i, lam) * jax.nn.gelu(ga)
    yb = bg * causal_dwconv(cg * hs, sc_conv_w)
    return jnp.concatenate([ya, yb], axis=-1) @ w_out


def odd_mixer(h, w_in, pool_w, pool_scale, sg_norm, sg_w, sg_b, w_out):
    z = h @ w_in
    xp, u, v = jnp.split(z, [POOL_WIDTH, POOL_WIDTH + SG_WIDTH], axis=-1)
    yc = multiscale_pool(xp, pool_w, pool_scale)
    yd = u * spatial_gate(rms_norm(v, sg_norm), sg_w, sg_b)
    return jnp.concatenate([yc, yd], axis=-1) @ w_out


def swiglu(h, w_gate, w_up, w_down):
    return (jax.nn.silu(h @ w_gate) * (h @ w_up)) @ w_down


def setup_inputs(seed: int = 0) -> dict:
    key = jax.random.key(seed)
    ks = iter(jax.random.split(key, 40))

    def nrm(shape, scale):
        return jax.random.normal(next(ks), shape, jnp.float32) * scale

    def gain(shape):
        return 1.0 + nrm(shape, 0.05)

    x = nrm((BATCH, SEQ, D_MODEL), 1.0)
    e_w_in = nrm((N_EVEN, D_MODEL, EVEN_IN), D_MODEL ** -0.5)
    e_conv_w = nrm((N_EVEN, LRU_CONV, LRU_WIDTH), LRU_CONV ** -0.5)
    e_conv_b = nrm((N_EVEN, LRU_WIDTH), 0.02)
    e_w_r = nrm((N_EVEN, LRU_HEADS, LRU_HEAD_DIM, LRU_HEAD_DIM), LRU_HEAD_DIM ** -0.5)
    e_b_r = nrm((N_EVEN, LRU_HEADS, LRU_HEAD_DIM), 0.02)
    e_w_i = nrm((N_EVEN, LRU_HEADS, LRU_HEAD_DIM, LRU_HEAD_DIM), LRU_HEAD_DIM ** -0.5)
    e_b_i = nrm((N_EVEN, LRU_HEADS, LRU_HEAD_DIM), 0.02)
    u = jax.random.uniform(next(ks), (N_EVEN, LRU_WIDTH), jnp.float32, 0.9, 0.999)
    a0 = u ** (1.0 / LRU_C)
    e_lam = jnp.log(a0) - jnp.log1p(-a0)
    e_sc_conv_w = nrm((N_EVEN, SC_CONV, SC_WIDTH), SC_CONV ** -0.5)
    e_w_out = nrm((N_EVEN, LRU_WIDTH + SC_WIDTH, D_MODEL), (LRU_WIDTH + SC_WIDTH) ** -0.5)

    o_w_in = nrm((N_ODD, D_MODEL, ODD_IN), D_MODEL ** -0.5)
    o_pool_w = nrm((N_ODD, len(POOL_WINDOWS), POOL_GROUP, POOL_GROUP), POOL_GROUP ** -0.5)
    o_pool_scale = gain((N_ODD, POOL_WIDTH))
    o_sg_norm = gain((N_ODD, SG_WIDTH))
    o_sg_w = nrm((N_ODD, SG_HEADS, SG_BLOCK, SG_BLOCK), SG_BLOCK ** -0.5)
    o_sg_b = gain((N_ODD, SG_HEADS, SG_BLOCK))
    o_w_out = nrm((N_ODD, POOL_WIDTH + SG_WIDTH, D_MODEL), (POOL_WIDTH + SG_WIDTH) ** -0.5)

    norm_mix_pre = gain((DEPTH, D_MODEL))
    norm_mix_post = gain((DEPTH, D_MODEL))
    norm_ffn_pre = gain((DEPTH, D_MODEL))
    norm_ffn_post = gain((DEPTH, D_MODEL))
    w_gate = nrm((DEPTH, D_MODEL, FFN_HIDDEN), D_MODEL ** -0.5)
    w_up = nrm((DEPTH, D_MODEL, FFN_HIDDEN), D_MODEL ** -0.5)
    w_down = nrm((DEPTH, FFN_HIDDEN, D_MODEL), FFN_HIDDEN ** -0.5)
    return {
        "x": x,
        "e_w_in": e_w_in, "e_conv_w": e_conv_w, "e_conv_b": e_conv_b,
        "e_w_r": e_w_r, "e_b_r": e_b_r, "e_w_i": e_w_i, "e_b_i": e_b_i,
        "e_lam": e_lam, "e_sc_conv_w": e_sc_conv_w, "e_w_out": e_w_out,
        "o_w_in": o_w_in, "o_pool_w": o_pool_w, "o_pool_scale": o_pool_scale,
        "o_sg_norm": o_sg_norm, "o_sg_w": o_sg_w, "o_sg_b": o_sg_b, "o_w_out": o_w_out,
        "norm_mix_pre": norm_mix_pre, "norm_mix_post": norm_mix_post,
        "norm_ffn_pre": norm_ffn_pre, "norm_ffn_post": norm_ffn_post,
        "w_gate": w_gate, "w_up": w_up, "w_down": w_down,
    }


def reference(x, e_w_in, e_conv_w, e_conv_b, e_w_r, e_b_r, e_w_i, e_b_i, e_lam,
              e_sc_conv_w, e_w_out, o_w_in, o_pool_w, o_pool_scale, o_sg_norm,
              o_sg_w, o_sg_b, o_w_out, norm_mix_pre, norm_mix_post, norm_ffn_pre,
              norm_ffn_post, w_gate, w_up, w_down):
    h = x
    for layer in range(DEPTH):
        hn = rms_norm(h, norm_mix_pre[layer])
        if layer % 2 == 0:
            j = layer // 2
            m = even_mixer(hn, e_w_in[j], e_conv_w[j], e_conv_b[j], e_w_r[j], e_b_r[j],
                           e_w_i[j], e_b_i[j], e_lam[j], e_sc_conv_w[j], e_w_out[j])
        else:
            j = layer // 2
            m = odd_mixer(hn, o_w_in[j], o_pool_w[j], o_pool_scale[j], o_sg_norm[j],
                          o_sg_w[j], o_sg_b[j], o_w_out[j])
        h = h + rms_norm(m, norm_mix_post[layer])
        f = swiglu(rms_norm(h, norm_ffn_pre[layer]), w_gate[layer], w_up[layer], w_down[layer])
        h = h + rms_norm(f, norm_ffn_post[layer])
    return h
```

```python
import functools

import jax
import jax.numpy as jnp
from jax import lax
from jax.experimental import pallas as pl
from jax.experimental.pallas import tpu as pltpu

RMS_EPS = 1e-6
LRU_C = 8.0
LRU_HEADS = 8
CHUNK = 64
SG_BLOCK = 128
SG_HEADS = 8
POOL_WINDOWS = (2, 4, 8, 16)
POOL_GROUP = 128

SUBLANES = 8
POOL_TAIL = 16

MIX_TILE = 256
FFN_TILE = 256
VMEM_LIMIT = 56 * 1024 * 1024

F32 = jnp.float32
BF16 = jnp.bfloat16


def _rms(x, g):
    ms = jnp.mean(x * x, axis=-1, keepdims=True)
    return x * lax.rsqrt(ms + RMS_EPS) * g


def _dot(a, b):
    return jnp.dot(a, b, preferred_element_type=F32)


def _causal_conv(buf, taps, n_rows):
    k_taps = taps.shape[0]
    acc = None
    for k in range(k_taps):
        term = taps[k:k + 1, :] * buf[pl.ds(SUBLANES - (k_taps - 1) + k, n_rows), :]
        acc = term if acc is None else acc + term
    return acc


def _linear_scan(a, b):
    n_rows = a.shape[0]
    row = lax.broadcasted_iota(jnp.int32, a.shape, 0)
    d = 1
    while d < n_rows:
        keep = row >= d
        b_sh = jnp.where(keep, pltpu.roll(b, d, 0), 0.0)
        b = a * b_sh + b
        if 2 * d < n_rows:
            a = a * jnp.where(keep, pltpu.roll(a, d, 0), 1.0)
        d *= 2
    return b


def _even_mixer_kernel(x_ref, gpre_ref, win_ref, convw_ref, convb_ref, wgate_ref, br_ref, bi_ref,
                       lam_ref, scw_ref, wout_ref, gpost_ref, o_ref, xa_buf, sc_buf, h_carry):
    t = x_ref.shape[0]
    w = xa_buf.shape[1]

    @pl.when(pl.program_id(1) == 0)
    def _():
        xa_buf[0:SUBLANES, :] = jnp.zeros((SUBLANES, w), F32)
        sc_buf[0:SUBLANES, :] = jnp.zeros((SUBLANES, w), F32)
        h_carry[...] = jnp.zeros_like(h_carry)

    x = x_ref[...]
    hn = _rms(x, gpre_ref[...]).astype(BF16)
    z = _dot(hn, win_ref[...])
    xa, ga, bg, cg, hs = (z[:, i * w:(i + 1) * w] for i in range(5))

    xa_buf[SUBLANES:SUBLANES + t, :] = xa
    xc = _causal_conv(xa_buf, convw_ref[...], t) + convb_ref[...]
    xa_buf[0:SUBLANES, :] = xa_buf[t:t + SUBLANES, :]

    xcb = xc.astype(BF16)
    half = w // 2
    g0 = _dot(xcb[:, :half], wgate_ref[0])
    g1 = _dot(xcb[:, half:], wgate_ref[1])
    r = jax.nn.sigmoid(jnp.concatenate([g0[:, :half], g1[:, :half]], axis=1) + br_ref[...])
    i = jax.nn.sigmoid(jnp.concatenate([g0[:, half:], g1[:, half:]], axis=1) + bi_ref[...])

    neg_lam = -lam_ref[...]
    softplus = jnp.maximum(neg_lam, 0.0) + jnp.log1p(jnp.exp(-jnp.abs(neg_lam)))
    log_a = -LRU_C * r * softplus
    a = jnp.exp(log_a)
    b = jnp.sqrt(-jnp.tanh(log_a) * (a * a + 1.0)) * (i * xc)
    row = lax.broadcasted_iota(jnp.int32, a.shape, 0)
    b = b + jnp.where(row == 0, a * h_carry[...], 0.0)
    h = _linear_scan(a, b)
    h_carry[...] = h[t - 1:t, :]
    ya = h * jax.nn.gelu(ga)

    sc_buf[SUBLANES:SUBLANES + t, :] = cg * hs
    yb = bg * _causal_conv(sc_buf, scw_ref[...], t)
    sc_buf[0:SUBLANES, :] = sc_buf[t:t + SUBLANES, :]

    y = jnp.concatenate([ya, yb], axis=1).astype(BF16)
    m = _dot(y, wout_ref[...])
    o_ref[...] = x + _rms(m, gpost_ref[...])


def _odd_mixer_kernel(x_ref, gpre_ref, win_ref, poolw_ref, pscale_ref, sgnorm_ref, sgw_ref, sgb_ref,
                      wout_ref, gpost_ref, o_ref, xp_buf):
    t = x_ref.shape[0]
    w = xp_buf.shape[1]
    s_idx = pl.program_id(1)

    @pl.when(s_idx == 0)
    def _():
        xp_buf[0:POOL_TAIL, :] = jnp.zeros((POOL_TAIL, w), F32)

    x = x_ref[...]
    hn = _rms(x, gpre_ref[...]).astype(BF16)
    z = _dot(hn, win_ref[...])
    xp, u, v = (z[:, i * w:(i + 1) * w] for i in range(3))

    xp_buf[POOL_TAIL:POOL_TAIL + t, :] = xp
    pos = s_idx * t + lax.broadcasted_iota(jnp.int32, (t, POOL_GROUP), 0)
    count = (pos + 1).astype(F32)
    yc_groups = []
    for g, win in enumerate(POOL_WINDOWS):
        lanes = slice(g * POOL_GROUP, (g + 1) * POOL_GROUP)
        wsum = xp[:, lanes]
        for k in range(1, win):
            wsum = wsum + xp_buf[pl.ds(POOL_TAIL - k, t), lanes]
        p = wsum / jnp.minimum(count, float(win)) - xp[:, lanes]
        yc_groups.append(_dot(p.astype(BF16), poolw_ref[g]))
    yc = jnp.concatenate(yc_groups, axis=1) * pscale_ref[...]
    xp_buf[0:POOL_TAIL, :] = xp_buf[t:t + POOL_TAIL, :]

    vn = _rms(v, sgnorm_ref[...]).astype(BF16)
    q_chunk = lax.broadcasted_iota(jnp.int32, (SG_BLOCK, SG_BLOCK), 0) // CHUNK
    k_chunk = lax.broadcasted_iota(jnp.int32, (SG_BLOCK, SG_BLOCK), 1) // CHUNK
    mask = k_chunk <= q_chunk
    ws = [jnp.where(mask, sgw_ref[hd], 0.0).astype(BF16) for hd in range(SG_HEADS)]
    head_dim = w // SG_HEADS
    first_head = lax.broadcasted_iota(jnp.int32, (SG_BLOCK, 2 * head_dim), 1) < head_dim
    g_blocks = []
    for blk in range(t // SG_BLOCK):
        vb = vn[blk * SG_BLOCK:(blk + 1) * SG_BLOCK, :]
        pairs = []
        for hp in range(SG_HEADS // 2):
            vpair = vb[:, hp * 2 * head_dim:(hp + 1) * 2 * head_dim]
            pairs.append(jnp.where(first_head, _dot(ws[2 * hp], vpair), _dot(ws[2 * hp + 1], vpair)))
        g_blocks.append(jnp.concatenate(pairs, axis=1) + sgb_ref[...])
    yd = u * jnp.concatenate(g_blocks, axis=0)

    y = jnp.concatenate([yc, yd], axis=1).astype(BF16)
    m = _dot(y, wout_ref[...])
    o_ref[...] = x + _rms(m, gpost_ref[...])


def _ffn_kernel(x_ref, gpre_ref, wg_ref, wu_ref, wd_ref, gpost_ref, o_ref):
    x = x_ref[...]
    hn = _rms(x, gpre_ref[...]).astype(BF16)
    gate = _dot(hn, wg_ref[...])
    up = _dot(hn, wu_ref[...])
    act = (jax.nn.silu(gate) * up).astype(BF16)
    f = _dot(act, wd_ref[...])
    o_ref[...] = x + _rms(f, gpost_ref[...])


def _resident(shape):
    zeros = (0,) * len(shape)
    return pl.BlockSpec(shape, lambda *_: zeros, pipeline_mode=pl.Buffered(1))


def _mixer_call(kernel_fn, name, h, operands, scratch_shapes):
    bsz, seq, d = h.shape
    tile = pl.BlockSpec((None, MIX_TILE, d), lambda b, s: (b, s, 0))
    return pl.pallas_call(
        kernel_fn,
        name=name,
        grid=(bsz, seq // MIX_TILE),
        in_specs=[tile] + [_resident(op.shape) for op in operands],
        out_specs=tile,
        out_shape=jax.ShapeDtypeStruct(h.shape, h.dtype),
        scratch_shapes=scratch_shapes,
        compiler_params=pltpu.CompilerParams(
            dimension_semantics=("arbitrary", "arbitrary"), vmem_limit_bytes=VMEM_LIMIT),
    )(h, *operands)


def _ffn_call(h, operands):
    bsz, seq, d = h.shape
    flat = h.reshape(bsz * seq, d)
    tile = pl.BlockSpec((FFN_TILE, d), lambda i: (i, 0))
    out = pl.pallas_call(
        _ffn_kernel,
        name="swiglu_ffn",
        grid=(bsz * seq // FFN_TILE,),
        in_specs=[tile] + [_resident(op.shape) for op in operands],
        out_specs=tile,
        out_shape=jax.ShapeDtypeStruct(flat.shape, flat.dtype),
        compiler_params=pltpu.CompilerParams(
            dimension_semantics=("arbitrary",), vmem_limit_bytes=VMEM_LIMIT),
    )(flat, *operands)
    return out.reshape(bsz, seq, d)


def _block_diag(wh, group):
    n_heads, d, e = wh.shape
    wg = wh.reshape(n_heads // group, group, d, e)
    eye = jnp.eye(group, dtype=wh.dtype)
    return jnp.einsum("gnde,nm->gndme", wg, eye).reshape(n_heads // group, group * d, group * e)


def _row(v):
    return v.reshape(1, -1)


def kernel(x, e_w_in, e_conv_w, e_conv_b, e_w_r, e_b_r, e_w_i, e_b_i, e_lam, e_sc_conv_w, e_w_out,
           o_w_in, o_pool_w, o_pool_scale, o_sg_norm, o_sg_w, o_sg_b, o_w_out, norm_mix_pre,
           norm_mix_post, norm_ffn_pre, norm_ffn_post, w_gate, w_up, w_down):
    depth = norm_mix_pre.shape[0]
    lru_w = e_conv_w.shape[-1]
    sg_w = o_sg_norm.shape[-1]
    h = x
    for layer in range(depth):
        j = layer // 2
        if layer % 2 == 0:
            gate_w = jnp.concatenate(
                [_block_diag(e_w_r[j], LRU_HEADS // 2), _block_diag(e_w_i[j], LRU_HEADS // 2)],
                axis=-1).astype(BF16)
            operands = (
                _row(norm_mix_pre[layer]), e_w_in[j].astype(BF16), e_conv_w[j], _row(e_conv_b[j]),
                gate_w, _row(e_b_r[j]), _row(e_b_i[j]), _row(e_lam[j]), e_sc_conv_w[j],
                e_w_out[j].astype(BF16), _row(norm_mix_post[layer]))
            scratch = [pltpu.VMEM((MIX_TILE + SUBLANES, lru_w), F32),
                       pltpu.VMEM((MIX_TILE + SUBLANES, lru_w), F32),
                       pltpu.VMEM((1, lru_w), F32)]
            h = _mixer_call(_even_mixer_kernel, "even_mixer", h, operands, scratch)
        else:
            sg_bias = jnp.repeat(o_sg_b[j].T, sg_w // SG_HEADS, axis=1)
            operands = (
                _row(norm_mix_pre[layer]), o_w_in[j].astype(BF16), o_pool_w[j].astype(BF16),
                _row(o_pool_scale[j]), _row(o_sg_norm[j]), o_sg_w[j], sg_bias,
                o_w_out[j].astype(BF16), _row(norm_mix_post[layer]))
            scratch = [pltpu.VMEM((MIX_TILE + POOL_TAIL, sg_w), F32)]
            h = _mixer_call(_odd_mixer_kernel, "odd_mixer", h, operands, scratch)
        ffn_operands = (_row(norm_ffn_pre[layer]), w_gate[layer].astype(BF16), w_up[layer].astype(BF16),
                        w_down[layer].astype(BF16), _row(norm_ffn_post[layer]))
        h = _ffn_call(h, ffn_operands)
    return h
```
